```python
import jax, jax.numpy as jnp
from jax import lax
import numpy as np

D_MODEL = 1024
BATCH = 16
SEQ = 2048
DEPTH = 4
DEC_BATCH = 8
DEC_SEQ = 2048
PAST_LEN = 128

GRID_W = 64
N_MIXERS = 3
EPS = 1e-6

NA_HEADS = 16
NA_HEAD_DIM = D_MODEL // NA_HEADS
NA_MAX_KH = 8
NA_KW = 16

POOL_WINDOWS = (2, 4, 8, 16)
POOL_GROUPS = 4
POOL_GROUP_DIM = D_MODEL // POOL_GROUPS

GLA_HEADS = 4
GLA_DK = D_MODEL // 2
GLA_DV = D_MODEL
GLA_HEAD_K = GLA_DK // GLA_HEADS
GLA_HEAD_V = GLA_DV // GLA_HEADS
GLA_GATE_RANK = 16
GLA_GATE_TAU = 16.0
GLA_CHUNK = 64
GLA_IN_WIDTH = 2 * GLA_DK + 2 * GLA_DV + 2 * GLA_GATE_RANK

N_EXPERTS = 16
EC_CAPACITY = 2
D_EXPERT = 2048

kernel_name = 'hybrid_bidir_encoder_na_pool_gla_ec'


def _mixer_counts():
    n_a = len(range(0, DEPTH, N_MIXERS))
    n_b = len(range(1, DEPTH, N_MIXERS))
    n_c = len(range(2, DEPTH, N_MIXERS))
    return n_a, n_b, n_c


def rmsnorm(x, g):
    xf = x.astype(jnp.float32)
    ms = jnp.mean(xf * xf, axis=-1, keepdims=True)
    return (xf * lax.rsqrt(ms + EPS) * g.astype(jnp.float32)).astype(x.dtype)


def neighbourhood_attention(h, w_qkv, rpb, w_o):
    B, T, _ = h.shape
    rows = T // GRID_W
    kh = min(NA_MAX_KH, rows)
    qkv = jnp.einsum('btd,de->bte', h, w_qkv)
    q, k, v = jnp.split(qkv, 3, axis=-1)
    grid = (B, rows, GRID_W, NA_HEADS, NA_HEAD_DIM)
    q = q.reshape(grid) * (NA_HEAD_DIM ** -0.5)
    k = k.reshape(grid)
    v = v.reshape(grid)
    cols = jnp.arange(GRID_W)
    col_start = jnp.clip(cols - NA_KW // 2, 0, GRID_W - NA_KW)
    col_idx = col_start[:, None] + jnp.arange(NA_KW)[None, :]
    dc = col_idx - cols[:, None] + (NA_KW - 1)

    def one_row(args):
        r, q_r = args
        r0 = jnp.clip(r - kh // 2, 0, rows - kh)
        k_rows = lax.dynamic_slice_in_dim(k, r0, kh, axis=1)
        v_rows = lax.dynamic_slice_in_dim(v, r0, kh, axis=1)
        k_win = k_rows[:, :, col_idx]
        v_win = v_rows[:, :, col_idx]
        dr = r0 + jnp.arange(kh) - r + (NA_MAX_KH - 1)
        bias = rpb[:, dr[:, None, None], dc[None, :, :]]
        bias = jnp.transpose(bias, (0, 2, 1, 3)).astype(jnp.float32)
        s = jnp.einsum('bchd,bicjhd->bhcij', q_r, k_win).astype(jnp.float32) + bias[None]
        p = jax.nn.softmax(s.reshape(B, NA_HEADS, GRID_W, kh * NA_KW), axis=-1)
        p = p.reshape(B, NA_HEADS, GRID_W, kh, NA_KW).astype(v.dtype)
        return jnp.einsum('bhcij,bicjhd->bchd', p, v_win)

    out = lax.map(one_row, (jnp.arange(rows), jnp.moveaxis(q, 1, 0)))
    out = jnp.moveaxis(out, 0, 1).reshape(B, T, D_MODEL)
    return jnp.einsum('btd,de->bte', out, w_o)


def multiscale_pool(h, pool_w, pool_scale):
    B, T, _ = h.shape
    hf = h.astype(jnp.float32)
    csum = jnp.concatenate([jnp.zeros((B, 1, D_MODEL), jnp.float32), jnp.cumsum(hf, axis=1)], axis=1)
    t = jnp.arange(T)
    groups = []
    for g, w in enumerate(POOL_WINDOWS):
        lo = jnp.clip(t - w // 2, 0, T)
        hi = jnp.clip(t + w - w // 2, 0, T)
        cs_g = csum[..., g * POOL_GROUP_DIM:(g + 1) * POOL_GROUP_DIM]
        cnt = (hi - lo).astype(jnp.float32)[None, :, None]
        mean = (cs_g[:, hi] - cs_g[:, lo]) / cnt
        groups.append(mean - hf[..., g * POOL_GROUP_DIM:(g + 1) * POOL_GROUP_DIM])
    pooled = jnp.stack(groups, axis=2)
    mixed = jnp.einsum('btgi,gio->btgo', pooled, pool_w.astype(jnp.float32)).reshape(B, T, D_MODEL)
    return (mixed * pool_scale.astype(jnp.float32)).astype(h.dtype)


def gla_chunked(q, k, v, log_a, include_diag):
    B, T, H, dk = q.shape
    dv = v.shape[-1]
    n = T // GLA_CHUNK
    q = q.reshape(B, n, GLA_CHUNK, H, dk)
    k = k.reshape(B, n, GLA_CHUNK, H, dk)
    v = v.reshape(B, n, GLA_CHUNK, H, dv)
    b = jnp.cumsum(log_a.reshape(B, n, GLA_CHUNK, H, dk), axis=2)
    b_last = b[:, :, -1:]
    q_dec = q * jnp.exp(b)
    k_inv = k * jnp.exp(-b)
    k_to_end = k * jnp.exp(b_last - b)
    scores = jnp.einsum('bnihk,bnjhk->bnhij', q_dec, k_inv)
    mask = jnp.tril(jnp.ones((GLA_CHUNK, GLA_CHUNK), dtype=bool), k=0 if include_diag else -1)
    o_intra = jnp.einsum('bnhij,bnjhv->bnihv', jnp.where(mask, scores, 0.0), v)
    kv = jnp.einsum('bnjhk,bnjhv->bnhkv', k_to_end, v)
    decay = jnp.exp(b_last[:, :, 0])

    def step(S, inp):
        kv_c, dec_c = inp
        return S * dec_c[..., None] + kv_c, S

    S0 = jnp.zeros((B, H, dk, dv), jnp.float32)
    _, S_prev = lax.scan(step, S0, (jnp.moveaxis(kv, 1, 0), jnp.moveaxis(decay, 1, 0)))
    S_prev = jnp.moveaxis(S_prev, 0, 1)
    o_inter = jnp.einsum('bnihk,bnhkv->bnihv', q_dec, S_prev)
    return (o_intra + o_inter).reshape(B, T, H, dv)


def gla_mixer(h, w_in, w_gate_f, b_gate_f, w_gate_b, b_gate_b, norm_g, w_o):
    B, T, _ = h.shape
    proj = jnp.einsum('btd,de->bte', h, w_in).astype(jnp.float32)
    s1, s2, s3, s4 = GLA_DK, 2 * GLA_DK, 2 * GLA_DK + GLA_DV, 2 * GLA_DK + 2 * GLA_DV
    q = proj[..., :s1].reshape(B, T, GLA_HEADS, GLA_HEAD_K) * (GLA_HEAD_K ** -0.5)
    k = proj[..., s1:s2].reshape(B, T, GLA_HEADS, GLA_HEAD_K)
    v = proj[..., s2:s3].reshape(B, T, GLA_HEADS, GLA_HEAD_V)
    r = proj[..., s3:s4]
    lr_f = proj[..., s4:s4 + GLA_GATE_RANK]
    lr_b = proj[..., s4 + GLA_GATE_RANK:]
    log_a_f = jax.nn.log_sigmoid(lr_f @ w_gate_f.astype(jnp.float32) + b_gate_f.astype(jnp.float32)) / GLA_GATE_TAU
    log_a_b = jax.nn.log_sigmoid(lr_b @ w_gate_b.astype(jnp.float32) + b_gate_b.astype(jnp.float32)) / GLA_GATE_TAU
    log_a_f = log_a_f.reshape(B, T, GLA_HEADS, GLA_HEAD_K)
    log_a_b = log_a_b.reshape(B, T, GLA_HEADS, GLA_HEAD_K)
    o_f = gla_chunked(q, k, v, log_a_f, True)
    fl = lambda a: jnp.flip(a, axis=1)
    o_b = fl(gla_chunked(fl(q), fl(k), fl(v), fl(log_a_b), False))
    o = o_f + o_b
    o = o * lax.rsqrt(jnp.mean(o * o, axis=-1, keepdims=True) + EPS)
    o = o * norm_g.astype(jnp.float32).reshape(GLA_HEADS, GLA_HEAD_V)
    o = o.reshape(B, T, GLA_DV) * jax.nn.silu(r)
    return jnp.einsum('btv,vd->btd', o.astype(h.dtype), w_o)


def expert_choice_ffn(h, router_w, w_gate, w_up, w_down):
    B, T, D = h.shape
    n = B * T
    cap = EC_CAPACITY * n // N_EXPERTS
    xf = h.reshape(n, D)
    aff = jax.nn.softmax(jnp.einsum('nd,de->ne', xf, router_w).astype(jnp.float32), axis=-1)
    gate, idx = lax.top_k(aff.T, cap)
    xs = xf[idx]
    hid = jax.nn.silu(jnp.einsum('ecd,edf->ecf', xs, w_gate)) * jnp.einsum('ecd,edf->ecf', xs, w_up)
    out = jnp.einsum('ecf,efd->ecd', hid, w_down) * gate[..., None].astype(xs.dtype)
    y = jnp.zeros_like(xf).at[idx.reshape(-1)].add(out.reshape(-1, D).astype(xf.dtype))
    return y.reshape(B, T, D)


def trunk(x, norm_mix_g, norm_ffn_g, final_norm_g, na_w_qkv, na_rpb, na_w_o, pool_w, pool_scale,
          gla_w_in, gla_w_gate_f, gla_b_gate_f, gla_w_gate_b, gla_b_gate_b, gla_norm_g, gla_w_o,
          router_w, moe_w_gate, moe_w_up, moe_w_down):
    h = x
    ia, ib, ic = 0, 0, 0
    for i in range(DEPTH):
        u = rmsnorm(h, norm_mix_g[i])
        kind = i % N_MIXERS
        if kind == 0:
            mix = neighbourhood_attention(u, na_w_qkv[ia], na_rpb[ia], na_w_o[ia])
            ia += 1
        elif kind == 1:
            mix = multiscale_pool(u, pool_w[ib], pool_scale[ib])
            ib += 1
        else:
            mix = gla_mixer(u, gla_w_in[ic], gla_w_gate_f[ic], gla_b_gate_f[ic], gla_w_gate_b[ic],
                            gla_b_gate_b[ic], gla_norm_g[ic], gla_w_o[ic])
            ic += 1
        h = h + mix.astype(h.dtype)
        u = rmsnorm(h, norm_ffn_g[i])
        h = h + expert_choice_ffn(u, router_w[i], moe_w_gate[i], moe_w_up[i], moe_w_down[i]).astype(h.dtype)
    return rmsnorm(h, final_norm_g)


def setup_inputs(seed: int = 0) -> dict:
    key = jax.random.key(seed)
    ks = jax.random.split(key, 24)
    n_a, n_b, n_c = _mixer_counts()
    nrm = jax.random.normal
    f32 = jnp.float32
    return {
        'x_prompt': nrm(ks[0], (BATCH, SEQ, D_MODEL), f32),
        'x_sample': nrm(ks[1], (DEC_BATCH, DEC_SEQ, D_MODEL), f32),
        'norm_mix_g': 1.0 + 0.02 * nrm(ks[2], (DEPTH, D_MODEL), f32),
        'norm_ffn_g': 1.0 + 0.02 * nrm(ks[3], (DEPTH, D_MODEL), f32),
        'final_norm_g': 1.0 + 0.02 * nrm(ks[4], (D_MODEL,), f32),
        'na_w_qkv': nrm(ks[5], (n_a, D_MODEL, 3 * D_MODEL), f32) * D_MODEL ** -0.5,
        'na_rpb': 0.02 * nrm(ks[6], (n_a, NA_HEADS, 2 * NA_MAX_KH - 1, 2 * NA_KW - 1), f32),
        'na_w_o': nrm(ks[7], (n_a, D_MODEL, D_MODEL), f32) * D_MODEL ** -0.5,
        'pool_w': nrm(ks[8], (n_b, POOL_GROUPS, POOL_GROUP_DIM, POOL_GROUP_DIM), f32) * POOL_GROUP_DIM ** -0.5,
        'pool_scale': 1.0 + 0.02 * nrm(ks[9], (n_b, D_MODEL), f32),
        'gla_w_in': nrm(ks[10], (n_c, D_MODEL, GLA_IN_WIDTH), f32) * D_MODEL ** -0.5,
        'gla_w_gate_f': nrm(ks[11], (n_c, GLA_GATE_RANK, GLA_DK), f32) * GLA_GATE_RANK ** -0.5,
        'gla_b_gate_f': 0.1 * nrm(ks[12], (n_c, GLA_DK), f32),
        'gla_w_gate_b': nrm(ks[13], (n_c, GLA_GATE_RANK, GLA_DK), f32) * GLA_GATE_RANK ** -0.5,
        'gla_b_gate_b': 0.1 * nrm(ks[14], (n_c, GLA_DK), f32),
        'gla_norm_g': 1.0 + 0.02 * nrm(ks[15], (n_c, GLA_DV), f32),
        'gla_w_o': nrm(ks[16], (n_c, GLA_DV, D_MODEL), f32) * GLA_DV ** -0.5,
        'router_w': nrm(ks[17], (DEPTH, D_MODEL, N_EXPERTS), f32) * D_MODEL ** -0.5,
        'moe_w_gate': nrm(ks[18], (DEPTH, N_EXPERTS, D_MODEL, D_EXPERT), f32) * D_MODEL ** -0.5,
        'moe_w_up': nrm(ks[19], (DEPTH, N_EXPERTS, D_MODEL, D_EXPERT), f32) * D_MODEL ** -0.5,
        'moe_w_down': nrm(ks[20], (DEPTH, N_EXPERTS, D_EXPERT, D_MODEL), f32) * D_EXPERT ** -0.5,
    }


def reference(x_prompt, x_sample, norm_mix_g, norm_ffn_g, final_norm_g, na_w_qkv, na_rpb, na_w_o,
              pool_w, pool_scale, gla_w_in, gla_w_gate_f, gla_b_gate_f, gla_w_gate_b, gla_b_gate_b,
              gla_norm_g, gla_w_o, router_w, moe_w_gate, moe_w_up, moe_w_down):
    y_prompt = trunk(x_prompt, norm_mix_g, norm_ffn_g, final_norm_g, na_w_qkv, na_rpb, na_w_o, pool_w,
                     pool_scale, gla_w_in, gla_w_gate_f, gla_b_gate_f, gla_w_gate_b, gla_b_gate_b,
                     gla_norm_g, gla_w_o, router_w, moe_w_gate, moe_w_up, moe_w_down)
    y_sample = trunk(x_sample, norm_mix_g, norm_ffn_g, final_norm_g, na_w_qkv, na_rpb, na_w_o, pool_w,
                     pool_scale, gla_w_in, gla_w_gate_f, gla_b_gate_f, gla_w_gate_b, gla_b_gate_b,
                     gla_norm_g, gla_w_o, router_w, moe_w_gate, moe_w_up, moe_w_down)
    return (y_prompt, y_sample)
```

```python
import jax
import jax.numpy as jnp
from jax import lax
from jax.experimental import pallas as pl
from jax.experimental.pallas import tpu as pltpu

F32 = jnp.float32
BF16 = jnp.bfloat16
I32 = jnp.int32

D_MODEL = 1024
SEQ = 2048
EPS = 1e-6

GRID_W = 64
GRID_ROWS = SEQ // GRID_W
NA_HEADS = 16
NA_HEAD_DIM = 64
NA_KH = 8
NA_KW = 16
NA_BAND = NA_KH * GRID_W

POOL_WINDOWS = (2, 4, 8, 16)
POOL_GROUP_DIM = 256

GLA_HEADS = 4
GLA_DK = 512
GLA_DV = 1024
GLA_HEAD_K = 128
GLA_HEAD_V = 256
GLA_RANK = 16
GLA_TAU = 16.0
GLA_CHUNK = 64

N_EXPERTS = 16
EC_CAPACITY = 2
D_EXPERT = 2048

TOKEN_TILE = 512
FFN_TILE = 512
ROUTER_LANES = 128

VMEM_LIMIT = 56 * 1024 * 1024

_NT = (((1,), (1,)), ((), ()))


def _params(n_axes=1):
    return pltpu.CompilerParams(
        dimension_semantics=("arbitrary",) * n_axes, vmem_limit_bytes=VMEM_LIMIT)


def _rms(x, g):
    ms = jnp.mean(x * x, axis=-1, keepdims=True)
    return x * lax.rsqrt(ms + EPS) * g


def _na_qkv_kernel(x_ref, g_ref, w_ref, q_ref, k_ref, v_ref):
    u = _rms(x_ref[...], g_ref[...]).astype(BF16)
    d = D_MODEL
    q = jnp.dot(u, w_ref[:, 0:d], preferred_element_type=F32)
    q_ref[...] = (q * (NA_HEAD_DIM ** -0.5)).astype(BF16)
    k_ref[...] = jnp.dot(u, w_ref[:, d:2 * d], preferred_element_type=F32).astype(BF16)
    v_ref[...] = jnp.dot(u, w_ref[:, 2 * d:3 * d], preferred_element_type=F32).astype(BF16)


def _na_qkv(h, g, w_bf16):
    n = h.shape[0]
    tm = TOKEN_TILE
    tok = pl.BlockSpec((tm, D_MODEL), lambda i: (i, 0))
    return pl.pallas_call(
        _na_qkv_kernel,
        grid=(n // tm,),
        in_specs=[tok, pl.BlockSpec((1, D_MODEL), lambda i: (0, 0)),
                  pl.BlockSpec((D_MODEL, 3 * D_MODEL), lambda i: (0, 0))],
        out_specs=[tok, tok, tok],
        out_shape=[jax.ShapeDtypeStruct((n, D_MODEL), BF16)] * 3,
        compiler_params=_params(),
        name="na_qkv",
    )(h, g.reshape(1, -1), w_bf16)


def _na_attn_kernel(q_ref, k_ref, v_ref, bias_ref, o_ref):
    lane = lax.broadcasted_iota(I32, (1, 2 * NA_HEAD_DIM), 1)
    head_lanes = [(lane < NA_HEAD_DIM).astype(F32), (lane >= NA_HEAD_DIM).astype(F32)]
    out_left = lax.broadcasted_iota(I32, (GRID_W, 2 * NA_HEAD_DIM), 1) < NA_HEAD_DIM

    def row(r, carry):
        r0 = jnp.clip(r - NA_KH // 2, 0, GRID_ROWS - NA_KH)
        delta = r - r0
        qs = pl.multiple_of(r * GRID_W, GRID_W)
        ks = pl.multiple_of(r0 * GRID_W, GRID_W)
        q2 = q_ref[pl.ds(qs, GRID_W), :].astype(F32)
        k2 = k_ref[pl.ds(ks, NA_BAND), :]
        v2 = v_ref[pl.ds(ks, NA_BAND), :]
        outs = []
        for hh in range(2):
            qh = (q2 * head_lanes[hh]).astype(BF16)
            s = lax.dot_general(qh, k2, _NT, preferred_element_type=F32)
            s = s + bias_ref[0, hh, delta]
            m = jnp.max(s, axis=-1, keepdims=True)
            p = jnp.exp(s - m)
            l = jnp.sum(p, axis=-1, keepdims=True)
            o = jnp.dot(p.astype(BF16), v2, preferred_element_type=F32)
            outs.append(o / l)
        o_ref[pl.ds(qs, GRID_W), :] = jnp.where(out_left, outs[0], outs[1]).astype(BF16)
        return carry

    lax.fori_loop(0, GRID_ROWS, row, 0)


def _na_bias_table(rpb):
    delta = jnp.arange(NA_KH)[:, None, None, None]
    c = jnp.arange(GRID_W)[None, :, None, None]
    i = jnp.arange(NA_KH)[None, None, :, None]
    kc = jnp.arange(GRID_W)[None, None, None, :]
    c0 = jnp.clip(c - NA_KW // 2, 0, GRID_W - NA_KW)
    valid = (kc >= c0) & (kc < c0 + NA_KW)
    dr = i - delta + (NA_KH - 1)
    dc = jnp.clip(kc - c + (NA_KW - 1), 0, 2 * NA_KW - 2)
    dr, dc, valid = jnp.broadcast_arrays(dr, dc, valid)
    b = rpb[:, dr, dc].astype(F32)
    b = jnp.where(valid[None], b, -1e30)
    return b.reshape(NA_HEADS // 2, 2, NA_KH, GRID_W, NA_BAND)


def _na_attn(q, k, v, bias):
    n = q.shape[0]
    nseq = n // SEQ
    blk = pl.BlockSpec((SEQ, 2 * NA_HEAD_DIM), lambda hp, b: (b, hp))
    return pl.pallas_call(
        _na_attn_kernel,
        grid=(NA_HEADS // 2, nseq),
        in_specs=[blk, blk, blk,
                  pl.BlockSpec((1, 2, NA_KH, GRID_W, NA_BAND), lambda hp, b: (hp, 0, 0, 0, 0))],
        out_specs=blk,
        out_shape=jax.ShapeDtypeStruct((n, D_MODEL), BF16),
        compiler_params=_params(2),
        name="na_attn",
    )(q, k, v, bias)


def _split_bf16(x):
    hi = x.astype(BF16)
    return hi, (x - hi.astype(F32)).astype(BF16)


def _proj_res_kernel(a_ref, w_ref, h_ref, g_ref, wr_ref, hn_ref, u_ref, lg_ref):
    hn = h_ref[...] + jnp.dot(a_ref[...], w_ref[...], preferred_element_type=F32)
    hn_ref[...] = hn
    u = _rms(hn, g_ref[...])
    u_ref[...] = u.astype(BF16)
    u_hi, u_lo = _split_bf16(u)
    w_hi, w_lo = _split_bf16(wr_ref[...])
    lg = jnp.dot(u_hi, w_hi, preferred_element_type=F32)
    lg = lg + jnp.dot(u_lo, w_hi, preferred_element_type=F32)
    lg = lg + jnp.dot(u_hi, w_lo, preferred_element_type=F32)
    lg_ref[...] = lg


def _proj_res(a, w_bf16, h, g_ffn, wr):
    n = h.shape[0]
    tm = TOKEN_TILE
    tok = pl.BlockSpec((tm, D_MODEL), lambda i: (i, 0))
    hn, u, lg = pl.pallas_call(
        _proj_res_kernel,
        grid=(n // tm,),
        in_specs=[tok, pl.BlockSpec((D_MODEL, D_MODEL), lambda i: (0, 0)), tok,
                  pl.BlockSpec((1, D_MODEL), lambda i: (0, 0)),
                  pl.BlockSpec((D_MODEL, ROUTER_LANES), lambda i: (0, 0))],
        out_specs=[tok, tok, pl.BlockSpec((tm, ROUTER_LANES), lambda i: (i, 0))],
        out_shape=[jax.ShapeDtypeStruct((n, D_MODEL), F32),
                   jax.ShapeDtypeStruct((n, D_MODEL), BF16),
                   jax.ShapeDtypeStruct((n, ROUTER_LANES), F32)],
        compiler_params=_params(),
        name="proj_res",
    )(a, w_bf16, h, g_ffn.reshape(1, -1), jnp.pad(wr, ((0, 0), (0, ROUTER_LANES - N_EXPERTS))))
    return hn, u, lg[:, :N_EXPERTS]


def _multiscale_pool(u, pool_w, pool_scale):
    b, t, _ = u.shape
    csum = jnp.concatenate([jnp.zeros((b, 1, D_MODEL), F32), jnp.cumsum(u, axis=1)], axis=1)
    ti = jnp.arange(t)
    groups = []
    for g, w in enumerate(POOL_WINDOWS):
        lo = jnp.clip(ti - w // 2, 0, t)
        hi = jnp.clip(ti + w - w // 2, 0, t)
        cs_g = csum[..., g * POOL_GROUP_DIM:(g + 1) * POOL_GROUP_DIM]
        cnt = (hi - lo).astype(F32)[None, :, None]
        mean = (cs_g[:, hi] - cs_g[:, lo]) / cnt
        groups.append(mean - u[..., g * POOL_GROUP_DIM:(g + 1) * POOL_GROUP_DIM])
    pooled = jnp.stack(groups, axis=2)
    mixed = jnp.einsum('btgi,gio->btgo', pooled, pool_w).reshape(b, t, D_MODEL)
    return mixed * pool_scale


def _gla_chunked(q, k, v, log_a, include_diag):
    bsz, t, nh, dk = q.shape
    dv = v.shape[-1]
    c = GLA_CHUNK
    n = t // c
    q = q.reshape(bsz, n, c, nh, dk)
    k = k.reshape(bsz, n, c, nh, dk)
    v = v.reshape(bsz, n, c, nh, dv)
    b = jnp.cumsum(log_a.reshape(bsz, n, c, nh, dk), axis=2)
    b_last = b[:, :, -1:]
    q_dec = q * jnp.exp(b)
    k_inv = k * jnp.exp(-b)
    k_to_end = k * jnp.exp(b_last - b)
    scores = jnp.einsum('bnihk,bnjhk->bnhij', q_dec, k_inv)
    mask = jnp.tril(jnp.ones((c, c), dtype=bool), k=0 if include_diag else -1)
    o_intra = jnp.einsum('bnhij,bnjhv->bnihv', jnp.where(mask, scores, 0.0), v)
    kv = jnp.einsum('bnjhk,bnjhv->bnhkv', k_to_end, v)
    decay = jnp.exp(b_last[:, :, 0])

    def step(s, inp):
        kv_c, dec_c = inp
        return s * dec_c[..., None] + kv_c, s

    s0 = jnp.zeros((bsz, nh, dk, dv), F32)
    _, s_prev = lax.scan(step, s0, (jnp.moveaxis(kv, 1, 0), jnp.moveaxis(decay, 1, 0)))
    s_prev = jnp.moveaxis(s_prev, 0, 1)
    o_inter = jnp.einsum('bnihk,bnhkv->bnihv', q_dec, s_prev)
    return (o_intra + o_inter).reshape(bsz, t, nh, dv)


def _gla_mixer_core(u, w_in, w_gate_f, b_gate_f, w_gate_b, b_gate_b, norm_g):
    bsz, t, _ = u.shape
    proj = jnp.einsum('btd,de->bte', u, w_in)
    s1, s2, s3, s4 = GLA_DK, 2 * GLA_DK, 2 * GLA_DK + GLA_DV, 2 * GLA_DK + 2 * GLA_DV
    q = proj[..., :s1].reshape(bsz, t, GLA_HEADS, GLA_HEAD_K) * (GLA_HEAD_K ** -0.5)
    k = proj[..., s1:s2].reshape(bsz, t, GLA_HEADS, GLA_HEAD_K)
    v = proj[..., s2:s3].reshape(bsz, t, GLA_HEADS, GLA_HEAD_V)
    r = proj[..., s3:s4]
    lr_f = proj[..., s4:s4 + GLA_RANK]
    lr_b = proj[..., s4 + GLA_RANK:]
    log_a_f = jax.nn.log_sigmoid(lr_f @ w_gate_f + b_gate_f) / GLA_TAU
    log_a_b = jax.nn.log_sigmoid(lr_b @ w_gate_b + b_gate_b) / GLA_TAU
    log_a_f = log_a_f.reshape(bsz, t, GLA_HEADS, GLA_HEAD_K)
    log_a_b = log_a_b.reshape(bsz, t, GLA_HEADS, GLA_HEAD_K)
    o_f = _gla_chunked(q, k, v, log_a_f, True)
    fl = lambda a: jnp.flip(a, axis=1)
    o_b = fl(_gla_chunked(fl(q), fl(k), fl(v), fl(log_a_b), False))
    o = o_f + o_b
    o = o * lax.rsqrt(jnp.mean(o * o, axis=-1, keepdims=True) + EPS)
    o = o * norm_g.reshape(GLA_HEADS, GLA_HEAD_V)
    return o.reshape(bsz, t, GLA_DV) * jax.nn.silu(r)


def _ffn_kernel(x_ref, wg_ref, wu_ref, wd_ref, o_ref):
    x = x_ref[0]
    g = jnp.dot(x, wg_ref[0], preferred_element_type=F32)
    u = jnp.dot(x, wu_ref[0], preferred_element_type=F32)
    hid = (g * jax.nn.sigmoid(g) * u).astype(BF16)
    o_ref[0] = jnp.dot(hid, wd_ref[0], preferred_element_type=F32)


def _ffn(xs, wg, wu, wd):
    ne, rows, _ = xs.shape
    tile = pl.BlockSpec((1, FFN_TILE, D_MODEL), lambda e, j: (e, j, 0))
    return pl.pallas_call(
        _ffn_kernel,
        grid=(ne, rows // FFN_TILE),
        in_specs=[tile,
                  pl.BlockSpec((1, D_MODEL, D_EXPERT), lambda e, j: (e, 0, 0)),
                  pl.BlockSpec((1, D_MODEL, D_EXPERT), lambda e, j: (e, 0, 0)),
                  pl.BlockSpec((1, D_EXPERT, D_MODEL), lambda e, j: (e, 0, 0))],
        out_specs=tile,
        out_shape=jax.ShapeDtypeStruct((ne, rows, D_MODEL), F32),
        compiler_params=_params(2),
        name="moe_ffn",
    )(xs, wg, wu, wd)


def _moe(hn, u, logits, groups, wg, wu, wd):
    aff_t = jax.nn.softmax(logits, axis=-1).T
    picks, xs = [], []
    for start, size in groups:
        cap = EC_CAPACITY * size // N_EXPERTS
        gate, idx = lax.top_k(aff_t[:, start:start + size], cap)
        picks.append((start, cap, gate, idx))
        xs.append(u[start:start + size][idx])
    ys = _ffn(jnp.concatenate(xs, axis=1), wg, wu, wd)
    y = jnp.zeros_like(hn)
    first = 0
    for start, cap, gate, idx in picks:
        out = ys[:, first:first + cap] * gate[..., None]
        y = y.at[start + idx.reshape(-1)].add(out.reshape(-1, D_MODEL))
        first += cap
    return hn + y


def _trunk(h, groups, norm_mix_g, norm_ffn_g, final_norm_g, na_w_qkv, na_rpb, na_w_o, pool_w, pool_scale,
           gla_w_in, gla_w_gate_f, gla_b_gate_f, gla_w_gate_b, gla_b_gate_b, gla_norm_g, gla_w_o,
           router_w, moe_w_gate, moe_w_up, moe_w_down):
    depth = norm_mix_g.shape[0]
    n = h.shape[0]
    nseq = n // SEQ
    ia = ib = ic = 0
    for i in range(depth):
        g_ffn = norm_ffn_g[i]
        kind = i % 3
        if kind == 0:
            q, k, v = _na_qkv(h, norm_mix_g[i], na_w_qkv[ia].astype(BF16))
            a = _na_attn(q, k, v, _na_bias_table(na_rpb[ia]))
            hn, u, logits = _proj_res(a, na_w_o[ia].astype(BF16), h, g_ffn, router_w[i])
            ia += 1
        else:
            um = _rms(h, norm_mix_g[i]).reshape(nseq, SEQ, D_MODEL)
            if kind == 1:
                hn = h + _multiscale_pool(um, pool_w[ib], pool_scale[ib]).reshape(n, D_MODEL)
                uf = _rms(hn, g_ffn)
                u = uf.astype(BF16)
                logits = jnp.dot(uf, router_w[i], precision=lax.Precision.HIGHEST)
                ib += 1
            else:
                a = _gla_mixer_core(um, gla_w_in[ic], gla_w_gate_f[ic], gla_b_gate_f[ic], gla_w_gate_b[ic],
                                    gla_b_gate_b[ic], gla_norm_g[ic])
                hn, u, logits = _proj_res(a.reshape(n, GLA_DV).astype(BF16), gla_w_o[ic].astype(BF16),
                                          h, g_ffn, router_w[i])
                ic += 1
        h = _moe(hn, u, logits, groups, moe_w_gate[i].astype(BF16), moe_w_up[i].astype(BF16),
                 moe_w_down[i].astype(BF16))
    return _rms(h, final_norm_g)


def kernel(x_prompt, x_sample, norm_mix_g, norm_ffn_g, final_norm_g, na_w_qkv, na_rpb, na_w_o, pool_w, pool_scale, gla_w_in, gla_w_gate_f, gla_b_gate_f, gla_w_gate_b, gla_b_gate_b, gla_norm_g, gla_w_o, router_w, moe_w_gate, moe_w_up, moe_w_down):
    n_p = x_prompt.shape[0] * x_prompt.shape[1]
    n_s = x_sample.shape[0] * x_sample.shape[1]
    h = jnp.concatenate([x_prompt.reshape(n_p, D_MODEL), x_sample.reshape(n_s, D_MODEL)], axis=0)
    y = _trunk(h, ((0, n_p), (n_p, n_s)), norm_mix_g, norm_ffn_g, final_norm_g, na_w_qkv, na_rpb, na_w_o,
               pool_w, pool_scale, gla_w_in, gla_w_gate_f, gla_b_gate_f, gla_w_gate_b, gla_b_gate_b,
               gla_norm_g, gla_w_o, router_w, moe_w_gate, moe_w_up, moe_w_down)
    return y[:n_p].reshape(x_prompt.shape), y[n_p:].reshape(x_sample.shape)
```

```python
import jax
import jax.numpy as jnp
from jax import lax
from jax.experimental import pallas as pl
from jax.experimental.pallas import tpu as pltpu

F32 = jnp.float32
BF16 = jnp.bfloat16
I32 = jnp.int32

D_MODEL = 1024
SEQ = 2048
EPS = 1e-6

GRID_W = 64
GRID_ROWS = SEQ // GRID_W
NA_HEADS = 16
NA_HEAD_DIM = 64
NA_KH = 8
NA_KW = 16
NA_BAND = NA_KH * GRID_W

POOL_WINDOWS = (2, 4, 8, 16)
POOL_GROUP_DIM = 256

GLA_HEADS = 4
GLA_DK = 512
GLA_DV = 1024
GLA_HEAD_K = 128
GLA_HEAD_V = 256
GLA_RANK = 16
GLA_TAU = 16.0
GLA_CHUNK = 64

N_EXPERTS = 16
EC_CAPACITY = 2
D_EXPERT = 2048

TOKEN_TILE = 512
FFN_TILE = 512
ROUTER_LANES = 128

VMEM_LIMIT = 56 * 1024 * 1024

_NT = (((1,), (1,)), ((), ()))


def _params(n_axes=1):
    return pltpu.CompilerParams(
        dimension_semantics=("arbitrary",) * n_axes, vmem_limit_bytes=VMEM_LIMIT)


def _rms(x, g):
    ms = jnp.mean(x * x, axis=-1, keepdims=True)
    return x * lax.rsqrt(ms + EPS) * g


def _na_qkv_kernel(x_ref, g_ref, w_ref, q_ref, k_ref, v_ref):
    u = _rms(x_ref[...], g_ref[...]).astype(BF16)
    d = D_MODEL
    q = jnp.dot(u, w_ref[:, 0:d], preferred_element_type=F32)
    q_ref[...] = (q * (NA_HEAD_DIM ** -0.5)).astype(BF16)
    k_ref[...] = jnp.dot(u, w_ref[:, d:2 * d], preferred_element_type=F32).astype(BF16)
    v_ref[...] = jnp.dot(u, w_ref[:, 2 * d:3 * d], preferred_element_type=F32).astype(BF16)


def _na_qkv(h, g, w_bf16):
    n = h.shape[0]
    tm = TOKEN_TILE
    tok = pl.BlockSpec((tm, D_MODEL), lambda i: (i, 0))
    return pl.pallas_call(
        _na_qkv_kernel,
        grid=(n // tm,),
        in_specs=[tok, pl.BlockSpec((1, D_MODEL), lambda i: (0, 0)),
                  pl.BlockSpec((D_MODEL, 3 * D_MODEL), lambda i: (0, 0))],
        out_specs=[tok, tok, tok],
        out_shape=[jax.ShapeDtypeStruct((n, D_MODEL), BF16)] * 3,
        compiler_params=_params(),
        name="na_qkv",
    )(h, g.reshape(1, -1), w_bf16)


def _na_attn_kernel(q_ref, k_ref, v_ref, bias_ref, o_ref):
    lane = lax.broadcasted_iota(I32, (1, 2 * NA_HEAD_DIM), 1)
    head_lanes = [(lane < NA_HEAD_DIM).astype(F32), (lane >= NA_HEAD_DIM).astype(F32)]
    out_left = lax.broadcasted_iota(I32, (GRID_W, 2 * NA_HEAD_DIM), 1) < NA_HEAD_DIM

    def row(r, carry):
        r0 = jnp.clip(r - NA_KH // 2, 0, GRID_ROWS - NA_KH)
        delta = r - r0
        qs = pl.multiple_of(r * GRID_W, GRID_W)
        ks = pl.multiple_of(r0 * GRID_W, GRID_W)
        q2 = q_ref[pl.ds(qs, GRID_W), :].astype(F32)
        k2 = k_ref[pl.ds(ks, NA_BAND), :]
        v2 = v_ref[pl.ds(ks, NA_BAND), :]
        outs = []
        for hh in range(2):
            qh = (q2 * head_lanes[hh]).astype(BF16)
            s = lax.dot_general(qh, k2, _NT, preferred_element_type=F32)
            s = s + bias_ref[0, hh, delta]
            m = jnp.max(s, axis=-1, keepdims=True)
            p = jnp.exp(s - m)
            l = jnp.sum(p, axis=-1, keepdims=True)
            o = jnp.dot(p.astype(BF16), v2, preferred_element_type=F32)
            outs.append(o / l)
        o_ref[pl.ds(qs, GRID_W), :] = jnp.where(out_left, outs[0], outs[1]).astype(BF16)
        return carry

    lax.fori_loop(0, GRID_ROWS, row, 0)


def _na_bias_table(rpb):
    delta = jnp.arange(NA_KH)[:, None, None, None]
    c = jnp.arange(GRID_W)[None, :, None, None]
    i = jnp.arange(NA_KH)[None, None, :, None]
    kc = jnp.arange(GRID_W)[None, None, None, :]
    c0 = jnp.clip(c - NA_KW // 2, 0, GRID_W - NA_KW)
    valid = (kc >= c0) & (kc < c0 + NA_KW)
    dr = i - delta + (NA_KH - 1)
    dc = jnp.clip(kc - c + (NA_KW - 1), 0, 2 * NA_KW - 2)
    dr, dc, valid = jnp.broadcast_arrays(dr, dc, valid)
    b = rpb[:, dr, dc].astype(F32)
    b = jnp.where(valid[None], b, -1e30)
    return b.reshape(NA_HEADS // 2, 2, NA_KH, GRID_W, NA_BAND)


def _na_attn(q, k, v, bias):
    n = q.shape[0]
    nseq = n // SEQ
    blk = pl.BlockSpec((SEQ, 2 * NA_HEAD_DIM), lambda hp, b: (b, hp))
    return pl.pallas_call(
        _na_attn_kernel,
        grid=(NA_HEADS // 2, nseq),
        in_specs=[blk, blk, blk,
                  pl.BlockSpec((1, 2, NA_KH, GRID_W, NA_BAND), lambda hp, b: (hp, 0, 0, 0, 0))],
        out_specs=blk,
        out_shape=jax.ShapeDtypeStruct((n, D_MODEL), BF16),
        compiler_params=_params(2),
        name="na_attn",
    )(q, k, v, bias)


def _split_bf16(x):
    hi = x.astype(BF16)
    return hi, (x - hi.astype(F32)).astype(BF16)


def _proj_res_kernel(a_ref, w_ref, h_ref, g_ref, wr_ref, hn_ref, u_ref, lg_ref):
    hn = h_ref[...] + jnp.dot(a_ref[...], w_ref[...], preferred_element_type=F32)
    hn_ref[...] = hn
    u = _rms(hn, g_ref[...])
    u_ref[...] = u.astype(BF16)
    u_hi, u_lo = _split_bf16(u)
    w_hi, w_lo = _split_bf16(wr_ref[...])
    lg = jnp.dot(u_hi, w_hi, preferred_element_type=F32)
    lg = lg + jnp.dot(u_lo, w_hi, preferred_element_type=F32)
    lg = lg + jnp.dot(u_hi, w_lo, preferred_element_type=F32)
    lg_ref[...] = lg


def _proj_res(a, w_bf16, h, g_ffn, wr):
    n = h.shape[0]
    tm = TOKEN_TILE
    tok = pl.BlockSpec((tm, D_MODEL), lambda i: (i, 0))
    hn, u, lg = pl.pallas_call(
        _proj_res_kernel,
        grid=(n // tm,),
        in_specs=[tok, pl.BlockSpec((D_MODEL, D_MODEL), lambda i: (0, 0)), tok,
                  pl.BlockSpec((1, D_MODEL), lambda i: (0, 0)),
                  pl.BlockSpec((D_MODEL, ROUTER_LANES), lambda i: (0, 0))],
        out_specs=[tok, tok, pl.BlockSpec((tm, ROUTER_LANES), lambda i: (i, 0))],
        out_shape=[jax.ShapeDtypeStruct((n, D_MODEL), F32),
                   jax.ShapeDtypeStruct((n, D_MODEL), BF16),
                   jax.ShapeDtypeStruct((n, ROUTER_LANES), F32)],
        compiler_params=_params(),
        name="proj_res",
    )(a, w_bf16, h, g_ffn.reshape(1, -1), jnp.pad(wr, ((0, 0), (0, ROUTER_LANES - N_EXPERTS))))
    return hn, u, lg[:, :N_EXPERTS]


def _multiscale_pool(u, pool_w, pool_scale):
    b, t, _ = u.shape
    csum = jnp.concatenate([jnp.zeros((b, 1, D_MODEL), F32), jnp.cumsum(u, axis=1)], axis=1)
    ti = jnp.arange(t)
    groups = []
    for g, w in enumerate(POOL_WINDOWS):
        lo = jnp.clip(ti - w // 2, 0, t)
        hi = jnp.clip(ti + w - w // 2, 0, t)
        cs_g = csum[..., g * POOL_GROUP_DIM:(g + 1) * POOL_GROUP_DIM]
        cnt = (hi - lo).astype(F32)[None, :, None]
        ahead, behind = w - w // 2, w // 2
        cs_hi = jnp.concatenate([cs_g, jnp.repeat(cs_g[:, -1:], ahead, axis=1)], axis=1)[:, ahead:ahead + t]
        cs_lo = jnp.concatenate([jnp.repeat(cs_g[:, :1], behind, axis=1), cs_g], axis=1)[:, :t]
        mean = (cs_hi - cs_lo) / cnt
        groups.append(mean - u[..., g * POOL_GROUP_DIM:(g + 1) * POOL_GROUP_DIM])
    pooled = jnp.stack(groups, axis=2)
    mixed = jnp.einsum('btgi,gio->btgo', pooled, pool_w).reshape(b, t, D_MODEL)
    return mixed * pool_scale


def _gla_chunked(q, k, v, log_a, backward):
    bsz, t, nh, dk = q.shape
    dv = v.shape[-1]
    c = GLA_CHUNK
    n = t // c
    q = q.reshape(bsz, n, c, nh, dk)
    k = k.reshape(bsz, n, c, nh, dk)
    v = v.reshape(bsz, n, c, nh, dv)
    la = log_a.reshape(bsz, n, c, nh, dk)
    b = jnp.cumsum(la, axis=2)
    if backward:
        b = b[:, :, -1:] - b + la
        b_last = b[:, :, :1]
        mask = jnp.triu(jnp.ones((c, c), dtype=bool), k=1)
    else:
        b_last = b[:, :, -1:]
        mask = jnp.tril(jnp.ones((c, c), dtype=bool), k=0)
    q_dec = q * jnp.exp(b)
    k_inv = k * jnp.exp(-b)
    k_to_end = k * jnp.exp(b_last - b)
    scores = jnp.einsum('bnihk,bnjhk->bnhij', q_dec, k_inv)
    o_intra = jnp.einsum('bnhij,bnjhv->bnihv', jnp.where(mask, scores, 0.0), v)
    kv = jnp.einsum('bnjhk,bnjhv->bnhkv', k_to_end, v)
    decay = jnp.exp(b_last[:, :, 0])

    def step(s, inp):
        kv_c, dec_c = inp
        return s * dec_c[..., None] + kv_c, s

    s0 = jnp.zeros((bsz, nh, dk, dv), F32)
    _, s_prev = lax.scan(step, s0, (jnp.moveaxis(kv, 1, 0), jnp.moveaxis(decay, 1, 0)), reverse=backward)
    s_prev = jnp.moveaxis(s_prev, 0, 1)
    o_inter = jnp.einsum('bnihk,bnhkv->bnihv', q_dec, s_prev)
    return (o_intra + o_inter).reshape(bsz, t, nh, dv)


def _gla_mixer_core(u, w_in, w_gate_f, b_gate_f, w_gate_b, b_gate_b, norm_g):
    bsz, t, _ = u.shape
    proj = jnp.einsum('btd,de->bte', u, w_in)
    s1, s2, s3, s4 = GLA_DK, 2 * GLA_DK, 2 * GLA_DK + GLA_DV, 2 * GLA_DK + 2 * GLA_DV
    q = proj[..., :s1].reshape(bsz, t, GLA_HEADS, GLA_HEAD_K) * (GLA_HEAD_K ** -0.5)
    k = proj[..., s1:s2].reshape(bsz, t, GLA_HEADS, GLA_HEAD_K)
    v = proj[..., s2:s3].reshape(bsz, t, GLA_HEADS, GLA_HEAD_V)
    r = proj[..., s3:s4]
    lr_f = proj[..., s4:s4 + GLA_RANK]
    lr_b = proj[..., s4 + GLA_RANK:]
    log_a_f = jax.nn.log_sigmoid(lr_f @ w_gate_f + b_gate_f) / GLA_TAU
    log_a_b = jax.nn.log_sigmoid(lr_b @ w_gate_b + b_gate_b) / GLA_TAU
    log_a_f = log_a_f.reshape(bsz, t, GLA_HEADS, GLA_HEAD_K)
    log_a_b = log_a_b.reshape(bsz, t, GLA_HEADS, GLA_HEAD_K)
    o = _gla_chunked(q, k, v, log_a_f, False) + _gla_chunked(q, k, v, log_a_b, True)
    o = o * lax.rsqrt(jnp.mean(o * o, axis=-1, keepdims=True) + EPS)
    o = o * norm_g.reshape(GLA_HEADS, GLA_HEAD_V)
    return o.reshape(bsz, t, GLA_DV) * jax.nn.silu(r)


def _ffn_kernel(x_ref, wg_ref, wu_ref, wd_ref, o_ref):
    x = x_ref[0]
    g = jnp.dot(x, wg_ref[0], preferred_element_type=F32)
    u = jnp.dot(x, wu_ref[0], preferred_element_type=F32)
    hid = (g * jax.nn.sigmoid(g) * u).astype(BF16)
    o_ref[0] = jnp.dot(hid, wd_ref[0], preferred_element_type=F32)


def _ffn(xs, wg, wu, wd):
    ne, rows, _ = xs.shape
    tile = pl.BlockSpec((1, FFN_TILE, D_MODEL), lambda e, j: (e, j, 0))
    return pl.pallas_call(
        _ffn_kernel,
        grid=(ne, rows // FFN_TILE),
        in_specs=[tile,
                  pl.BlockSpec((1, D_MODEL, D_EXPERT), lambda e, j: (e, 0, 0)),
                  pl.BlockSpec((1, D_MODEL, D_EXPERT), lambda e, j: (e, 0, 0)),
                  pl.BlockSpec((1, D_EXPERT, D_MODEL), lambda e, j: (e, 0, 0))],
        out_specs=tile,
        out_shape=jax.ShapeDtypeStruct((ne, rows, D_MODEL), F32),
        compiler_params=_params(2),
        name="moe_ffn",
    )(xs, wg, wu, wd)


def _moe(hn, u, logits, groups, wg, wu, wd):
    aff_t = jax.nn.softmax(logits, axis=-1).T
    picks, xs = [], []
    for start, size in groups:
        cap = EC_CAPACITY * size // N_EXPERTS
        gate, idx = lax.top_k(aff_t[:, start:start + size], cap)
        picks.append((start, cap, gate, idx))
        xs.append(u[start:start + size][idx])
    ys = _ffn(jnp.concatenate(xs, axis=1), wg, wu, wd)
    y = jnp.zeros_like(hn)
    first = 0
    for start, cap, gate, idx in picks:
        out = ys[:, first:first + cap] * gate[..., None]
        y = y.at[start + idx.reshape(-1)].add(out.reshape(-1, D_MODEL))
        first += cap
    return hn + y


def _trunk(h, groups, norm_mix_g, norm_ffn_g, final_norm_g, na_w_qkv, na_rpb, na_w_o, pool_w, pool_scale,
           gla_w_in, gla_w_gate_f, gla_b_gate_f, gla_w_gate_b, gla_b_gate_b, gla_norm_g, gla_w_o,
           router_w, moe_w_gate, moe_w_up, moe_w_down):
    depth = norm_mix_g.shape[0]
    n = h.shape[0]
    nseq = n // SEQ
    ia = ib = ic = 0
    for i in range(depth):
        g_ffn = norm_ffn_g[i]
        kind = i % 3
        if kind == 0:
            q, k, v = _na_qkv(h, norm_mix_g[i], na_w_qkv[ia].astype(BF16))
            a = _na_attn(q, k, v, _na_bias_table(na_rpb[ia]))
            hn, u, logits = _proj_res(a, na_w_o[ia].astype(BF16), h, g_ffn, router_w[i])
            ia += 1
        else:
            um = _rms(h, norm_mix_g[i]).reshape(nseq, SEQ, D_MODEL)
            if kind == 1:
                hn = h + _multiscale_pool(um, pool_w[ib], pool_scale[ib]).reshape(n, D_MODEL)
                uf = _rms(hn, g_ffn)
                u = uf.astype(BF16)
                logits = jnp.dot(uf, router_w[i], precision=lax.Precision.HIGHEST)
                ib += 1
            else:
                a = _gla_mixer_core(um, gla_w_in[ic], gla_w_gate_f[ic], gla_b_gate_f[ic], gla_w_gate_b[ic],
                                    gla_b_gate_b[ic], gla_norm_g[ic])
                hn, u, logits = _proj_res(a.reshape(n, GLA_DV).astype(BF16), gla_w_o[ic].astype(BF16),
                                          h, g_ffn, router_w[i])
                ic += 1
        h = _moe(hn, u, logits, groups, moe_w_gate[i].astype(BF16), moe_w_up[i].astype(BF16),
                 moe_w_down[i].astype(BF16))
    return _rms(h, final_norm_g)


def kernel(x_prompt, x_sample, norm_mix_g, norm_ffn_g, final_norm_g, na_w_qkv, na_rpb, na_w_o, pool_w, pool_scale, gla_w_in, gla_w_gate_f, gla_b_gate_f, gla_w_gate_b, gla_b_gate_b, gla_norm_g, gla_w_o, router_w, moe_w_gate, moe_w_up, moe_w_down):
    n_p = x_prompt.shape[0] * x_prompt.shape[1]
    n_s = x_sample.shape[0] * x_sample.shape[1]
    h = jnp.concatenate([x_prompt.reshape(n_p, D_MODEL), x_sample.reshape(n_s, D_MODEL)], axis=0)
    y = _trunk(h, ((0, n_p), (n_p, n_s)), norm_mix_g, norm_ffn_g, final_norm_g, na_w_qkv, na_rpb, na_w_o,
               pool_w, pool_scale, gla_w_in, gla_w_gate_f, gla_b_gate_f, gla_w_gate_b, gla_b_gate_b,
               gla_norm_g, gla_w_o, router_w, moe_w_gate, moe_w_up, moe_w_down)
    return y[:n_p].reshape(x_prompt.shape), y[n_p:].reshape(x_sample.shape)
```

```python
import jax
import jax.numpy as jnp
from jax import lax
from jax.experimental import pallas as pl
from jax.experimental.pallas import tpu as pltpu

F32 = jnp.float32
BF16 = jnp.bfloat16
I32 = jnp.int32

D_MODEL = 1024
SEQ = 2048
EPS = 1e-6

GRID_W = 64
GRID_ROWS = SEQ // GRID_W
NA_HEADS = 16
NA_HEAD_DIM = 64
NA_KH = 8
NA_KW = 16
NA_BAND = NA_KH * GRID_W

POOL_WINDOWS = (2, 4, 8, 16)
POOL_GROUP_DIM = 256

GLA_HEADS = 4
GLA_DK = 512
GLA_DV = 1024
GLA_HEAD_K = 128
GLA_HEAD_V = 256
GLA_RANK = 16
GLA_TAU = 16.0
GLA_CHUNK = 64

N_EXPERTS = 16
EC_CAPACITY = 2
D_EXPERT = 2048

TOKEN_TILE = 512
FFN_TILE = 512
ROUTER_LANES = 128

VMEM_LIMIT = 56 * 1024 * 1024

_NT = (((1,), (1,)), ((), ()))


def _params(n_axes=1):
    return pltpu.CompilerParams(
        dimension_semantics=("arbitrary",) * n_axes, vmem_limit_bytes=VMEM_LIMIT)


def _rms(x, g):
    ms = jnp.mean(x * x, axis=-1, keepdims=True)
    return x * lax.rsqrt(ms + EPS) * g


def _na_qkv_kernel(x_ref, g_ref, w_ref, q_ref, k_ref, v_ref):
    u = _rms(x_ref[...], g_ref[...]).astype(BF16)
    d = D_MODEL
    q = jnp.dot(u, w_ref[:, 0:d], preferred_element_type=F32)
    q_ref[...] = (q * (NA_HEAD_DIM ** -0.5)).astype(BF16)
    k_ref[...] = jnp.dot(u, w_ref[:, d:2 * d], preferred_element_type=F32).astype(BF16)
    v_ref[...] = jnp.dot(u, w_ref[:, 2 * d:3 * d], preferred_element_type=F32).astype(BF16)


def _na_qkv(h, g, w_bf16):
    n = h.shape[0]
    tm = TOKEN_TILE
    tok = pl.BlockSpec((tm, D_MODEL), lambda i: (i, 0))
    return pl.pallas_call(
        _na_qkv_kernel,
        grid=(n // tm,),
        in_specs=[tok, pl.BlockSpec((1, D_MODEL), lambda i: (0, 0)),
                  pl.BlockSpec((D_MODEL, 3 * D_MODEL), lambda i: (0, 0))],
        out_specs=[tok, tok, tok],
        out_shape=[jax.ShapeDtypeStruct((n, D_MODEL), BF16)] * 3,
        compiler_params=_params(),
        name="na_qkv",
    )(h, g.reshape(1, -1), w_bf16)


def _na_attn_kernel(q_ref, k_ref, v_ref, bias_ref, o_ref):
    lane = lax.broadcasted_iota(I32, (1, 2 * NA_HEAD_DIM), 1)
    head_lanes = [(lane < NA_HEAD_DIM).astype(F32), (lane >= NA_HEAD_DIM).astype(F32)]
    out_left = lax.broadcasted_iota(I32, (GRID_W, 2 * NA_HEAD_DIM), 1) < NA_HEAD_DIM

    def row(r, carry):
        r0 = jnp.clip(r - NA_KH // 2, 0, GRID_ROWS - NA_KH)
        delta = r - r0
        qs = pl.multiple_of(r * GRID_W, GRID_W)
        ks = pl.multiple_of(r0 * GRID_W, GRID_W)
        q2 = q_ref[pl.ds(qs, GRID_W), :].astype(F32)
        k2 = k_ref[pl.ds(ks, NA_BAND), :]
        v2 = v_ref[pl.ds(ks, NA_BAND), :]
        outs = []
        for hh in range(2):
            qh = (q2 * head_lanes[hh]).astype(BF16)
            s = lax.dot_general(qh, k2, _NT, preferred_element_type=F32)
            s = s + bias_ref[0, hh, delta]
            m = jnp.max(s, axis=-1, keepdims=True)
            p = jnp.exp(s - m)
            l = jnp.sum(p, axis=-1, keepdims=True)
            o = jnp.dot(p.astype(BF16), v2, preferred_element_type=F32)
            outs.append(o / l)
        o_ref[pl.ds(qs, GRID_W), :] = jnp.where(out_left, outs[0], outs[1]).astype(BF16)
        return carry

    lax.fori_loop(0, GRID_ROWS, row, 0)


def _na_bias_table(rpb):
    delta = jnp.arange(NA_KH)[:, None, None, None]
    c = jnp.arange(GRID_W)[None, :, None, None]
    i = jnp.arange(NA_KH)[None, None, :, None]
    kc = jnp.arange(GRID_W)[None, None, None, :]
    c0 = jnp.clip(c - NA_KW // 2, 0, GRID_W - NA_KW)
    valid = (kc >= c0) & (kc < c0 + NA_KW)
    dr = (i - delta + (NA_KH - 1))[:, 0, :, 0]
    dc = jnp.clip(kc - c + (NA_KW - 1), 0, 2 * NA_KW - 2)[0, :, 0, :]
    rows = rpb.astype(F32)[:, dr]
    pick = (dc[None] == jnp.arange(2 * NA_KW - 1)[:, None, None]).astype(F32)
    b = jnp.einsum('hdix,xck->hdcik', rows, pick, precision=lax.Precision.HIGHEST)
    b = jnp.where(valid[None, :, :, :, :], b, -1e30)
    return b.reshape(NA_HEADS // 2, 2, NA_KH, GRID_W, NA_BAND)


def _na_attn(q, k, v, bias):
    n = q.shape[0]
    nseq = n // SEQ
    blk = pl.BlockSpec((SEQ, 2 * NA_HEAD_DIM), lambda hp, b: (b, hp))
    return pl.pallas_call(
        _na_attn_kernel,
        grid=(NA_HEADS // 2, nseq),
        in_specs=[blk, blk, blk,
                  pl.BlockSpec((1, 2, NA_KH, GRID_W, NA_BAND), lambda hp, b: (hp, 0, 0, 0, 0))],
        out_specs=blk,
        out_shape=jax.ShapeDtypeStruct((n, D_MODEL), BF16),
        compiler_params=_params(2),
        name="na_attn",
    )(q, k, v, bias)


def _split_bf16(x):
    hi = x.astype(BF16)
    return hi, (x - hi.astype(F32)).astype(BF16)


def _proj_res_kernel(a_ref, w_ref, h_ref, g_ref, wr_ref, hn_ref, u_ref, lg_ref):
    hn = h_ref[...] + jnp.dot(a_ref[...], w_ref[...], preferred_element_type=F32)
    hn_ref[...] = hn
    u = _rms(hn, g_ref[...])
    u_ref[...] = u.astype(BF16)
    u_hi, u_lo = _split_bf16(u)
    w_hi, w_lo = _split_bf16(wr_ref[...])
    lg = jnp.dot(u_hi, w_hi, preferred_element_type=F32)
    lg = lg + jnp.dot(u_lo, w_hi, preferred_element_type=F32)
    lg = lg + jnp.dot(u_hi, w_lo, preferred_element_type=F32)
    lg_ref[...] = lg


def _proj_res(a, w_bf16, h, g_ffn, wr):
    n = h.shape[0]
    tm = TOKEN_TILE
    tok = pl.BlockSpec((tm, D_MODEL), lambda i: (i, 0))
    hn, u, lg = pl.pallas_call(
        _proj_res_kernel,
        grid=(n // tm,),
        in_specs=[tok, pl.BlockSpec((D_MODEL, D_MODEL), lambda i: (0, 0)), tok,
                  pl.BlockSpec((1, D_MODEL), lambda i: (0, 0)),
                  pl.BlockSpec((D_MODEL, ROUTER_LANES), lambda i: (0, 0))],
        out_specs=[tok, tok, pl.BlockSpec((tm, ROUTER_LANES), lambda i: (i, 0))],
        out_shape=[jax.ShapeDtypeStruct((n, D_MODEL), F32),
                   jax.ShapeDtypeStruct((n, D_MODEL), BF16),
                   jax.ShapeDtypeStruct((n, ROUTER_LANES), F32)],
        compiler_params=_params(),
        name="proj_res",
    )(a, w_bf16, h, g_ffn.reshape(1, -1), jnp.pad(wr, ((0, 0), (0, ROUTER_LANES - N_EXPERTS))))
    return hn, u, lg[:, :N_EXPERTS]


def _multiscale_pool(u, pool_w, pool_scale):
    b, t, _ = u.shape
    csum = jnp.concatenate([jnp.zeros((b, 1, D_MODEL), F32), jnp.cumsum(u, axis=1)], axis=1)
    ti = jnp.arange(t)
    groups = []
    for g, w in enumerate(POOL_WINDOWS):
        lo = jnp.clip(ti - w // 2, 0, t)
        hi = jnp.clip(ti + w - w // 2, 0, t)
        cs_g = csum[..., g * POOL_GROUP_DIM:(g + 1) * POOL_GROUP_DIM]
        cnt = (hi - lo).astype(F32)[None, :, None]
        ahead, behind = w - w // 2, w // 2
        cs_hi = jnp.concatenate([cs_g, jnp.repeat(cs_g[:, -1:], ahead, axis=1)], axis=1)[:, ahead:ahead + t]
        cs_lo = jnp.concatenate([jnp.repeat(cs_g[:, :1], behind, axis=1), cs_g], axis=1)[:, :t]
        mean = (cs_hi - cs_lo) / cnt
        groups.append(mean - u[..., g * POOL_GROUP_DIM:(g + 1) * POOL_GROUP_DIM])
    pooled = jnp.stack(groups, axis=2)
    mixed = jnp.einsum('btgi,gio->btgo', pooled, pool_w).reshape(b, t, D_MODEL)
    return mixed * pool_scale


def _gla_chunked(q, k, v, log_a, backward):
    bsz, t, nh, dk = q.shape
    dv = v.shape[-1]
    c = GLA_CHUNK
    n = t // c
    q = q.reshape(bsz, n, c, nh, dk)
    k = k.reshape(bsz, n, c, nh, dk)
    v = v.reshape(bsz, n, c, nh, dv)
    la = log_a.reshape(bsz, n, c, nh, dk)
    b = jnp.cumsum(la, axis=2)
    if backward:
        b = b[:, :, -1:] - b + la
        b_last = b[:, :, :1]
        mask = jnp.triu(jnp.ones((c, c), dtype=bool), k=1)
    else:
        b_last = b[:, :, -1:]
        mask = jnp.tril(jnp.ones((c, c), dtype=bool), k=0)
    q_dec = q * jnp.exp(b)
    k_inv = k * jnp.exp(-b)
    k_to_end = k * jnp.exp(b_last - b)
    scores = jnp.einsum('bnihk,bnjhk->bnhij', q_dec, k_inv)
    o_intra = jnp.einsum('bnhij,bnjhv->bnihv', jnp.where(mask, scores, 0.0), v)
    kv = jnp.einsum('bnjhk,bnjhv->bnhkv', k_to_end, v)
    decay = jnp.exp(b_last[:, :, 0])

    def step(s, inp):
        kv_c, dec_c = inp
        return s * dec_c[..., None] + kv_c, s

    s0 = jnp.zeros((bsz, nh, dk, dv), F32)
    _, s_prev = lax.scan(step, s0, (jnp.moveaxis(kv, 1, 0), jnp.moveaxis(decay, 1, 0)), reverse=backward)
    s_prev = jnp.moveaxis(s_prev, 0, 1)
    o_inter = jnp.einsum('bnihk,bnhkv->bnihv', q_dec, s_prev)
    return (o_intra + o_inter).reshape(bsz, t, nh, dv)


def _gla_mixer_core(u, w_in, w_gate_f, b_gate_f, w_gate_b, b_gate_b, norm_g):
    bsz, t, _ = u.shape
    proj = jnp.einsum('btd,de->bte', u, w_in)
    s1, s2, s3, s4 = GLA_DK, 2 * GLA_DK, 2 * GLA_DK + GLA_DV, 2 * GLA_DK + 2 * GLA_DV
    q = proj[..., :s1].reshape(bsz, t, GLA_HEADS, GLA_HEAD_K) * (GLA_HEAD_K ** -0.5)
    k = proj[..., s1:s2].reshape(bsz, t, GLA_HEADS, GLA_HEAD_K)
    v = proj[..., s2:s3].reshape(bsz, t, GLA_HEADS, GLA_HEAD_V)
    r = proj[..., s3:s4]
    lr_f = proj[..., s4:s4 + GLA_RANK]
    lr_b = proj[..., s4 + GLA_RANK:]
    log_a_f = jax.nn.log_sigmoid(lr_f @ w_gate_f + b_gate_f) / GLA_TAU
    log_a_b = jax.nn.log_sigmoid(lr_b @ w_gate_b + b_gate_b) / GLA_TAU
    log_a_f = log_a_f.reshape(bsz, t, GLA_HEADS, GLA_HEAD_K)
    log_a_b = log_a_b.reshape(bsz, t, GLA_HEADS, GLA_HEAD_K)
    o = _gla_chunked(q, k, v, log_a_f, False) + _gla_chunked(q, k, v, log_a_b, True)
    o = o * lax.rsqrt(jnp.mean(o * o, axis=-1, keepdims=True) + EPS)
    o = o * norm_g.reshape(GLA_HEADS, GLA_HEAD_V)
    return o.reshape(bsz, t, GLA_DV) * jax.nn.silu(r)


def _ffn_kernel(x_ref, wg_ref, wu_ref, wd_ref, o_ref):
    x = x_ref[0]
    g = jnp.dot(x, wg_ref[0], preferred_element_type=F32)
    u = jnp.dot(x, wu_ref[0], preferred_element_type=F32)
    hid = (g * jax.nn.sigmoid(g) * u).astype(BF16)
    o_ref[0] = jnp.dot(hid, wd_ref[0], preferred_element_type=F32)


def _ffn(xs, wg, wu, wd):
    ne, rows, _ = xs.shape
    tile = pl.BlockSpec((1, FFN_TILE, D_MODEL), lambda e, j: (e, j, 0))
    return pl.pallas_call(
        _ffn_kernel,
        grid=(ne, rows // FFN_TILE),
        in_specs=[tile,
                  pl.BlockSpec((1, D_MODEL, D_EXPERT), lambda e, j: (e, 0, 0)),
                  pl.BlockSpec((1, D_MODEL, D_EXPERT), lambda e, j: (e, 0, 0)),
                  pl.BlockSpec((1, D_EXPERT, D_MODEL), lambda e, j: (e, 0, 0))],
        out_specs=tile,
        out_shape=jax.ShapeDtypeStruct((ne, rows, D_MODEL), F32),
        compiler_params=_params(2),
        name="moe_ffn",
    )(xs, wg, wu, wd)


def _moe(hn, u, logits, groups, wg, wu, wd):
    aff_t = jax.nn.softmax(logits, axis=-1).T
    picks, xs = [], []
    for start, size in groups:
        cap = EC_CAPACITY * size // N_EXPERTS
        gate, idx = lax.top_k(aff_t[:, start:start + size], cap)
        picks.append((start, cap, gate, idx))
        xs.append(u[start:start + size][idx])
    ys = _ffn(jnp.concatenate(xs, axis=1), wg, wu, wd)
    y = jnp.zeros_like(hn)
    first = 0
    for start, cap, gate, idx in picks:
        out = ys[:, first:first + cap] * gate[..., None]
        y = y.at[start + idx.reshape(-1)].add(out.reshape(-1, D_MODEL))
        first += cap
    return hn + y


def _trunk(h, groups, norm_mix_g, norm_ffn_g, final_norm_g, na_w_qkv, na_rpb, na_w_o, pool_w, pool_scale,
           gla_w_in, gla_w_gate_f, gla_b_gate_f, gla_w_gate_b, gla_b_gate_b, gla_norm_g, gla_w_o,
           router_w, moe_w_gate, moe_w_up, moe_w_down):
    depth = norm_mix_g.shape[0]
    n = h.shape[0]
    nseq = n // SEQ
    ia = ib = ic = 0
    for i in range(depth):
        g_ffn = norm_ffn_g[i]
        kind = i % 3
        if kind == 0:
            q, k, v = _na_qkv(h, norm_mix_g[i], na_w_qkv[ia].astype(BF16))
            a = _na_attn(q, k, v, _na_bias_table(na_rpb[ia]))
            hn, u, logits = _proj_res(a, na_w_o[ia].astype(BF16), h, g_ffn, router_w[i])
            ia += 1
        else:
            um = _rms(h, norm_mix_g[i]).reshape(nseq, SEQ, D_MODEL)
            if kind == 1:
                hn = h + _multiscale_pool(um, pool_w[ib], pool_scale[ib]).reshape(n, D_MODEL)
                uf = _rms(hn, g_ffn)
                u = uf.astype(BF16)
                logits = jnp.dot(uf, router_w[i], precision=lax.Precision.HIGHEST)
                ib += 1
            else:
                a = _gla_mixer_core(um, gla_w_in[ic], gla_w_gate_f[ic], gla_b_gate_f[ic], gla_w_gate_b[ic],
                                    gla_b_gate_b[ic], gla_norm_g[ic])
                hn, u, logits = _proj_res(a.reshape(n, GLA_DV).astype(BF16), gla_w_o[ic].astype(BF16),
                                          h, g_ffn, router_w[i])
                ic += 1
        h = _moe(hn, u, logits, groups, moe_w_gate[i].astype(BF16), moe_w_up[i].astype(BF16),
                 moe_w_down[i].astype(BF16))
    return _rms(h, final_norm_g)


def kernel(x_prompt, x_sample, norm_mix_g, norm_ffn_g, final_norm_g, na_w_qkv, na_rpb, na_w_o, pool_w, pool_scale, gla_w_in, gla_w_gate_f, gla_b_gate_f, gla_w_gate_b, gla_b_gate_b, gla_norm_g, gla_w_o, router_w, moe_w_gate, moe_w_up, moe_w_down):
    n_p = x_prompt.shape[0] * x_prompt.shape[1]
    n_s = x_sample.shape[0] * x_sample.shape[1]
    h = jnp.concatenate([x_prompt.reshape(n_p, D_MODEL), x_sample.reshape(n_s, D_MODEL)], axis=0)
    y = _trunk(h, ((0, n_p), (n_p, n_s)), norm_mix_g, norm_ffn_g, final_norm_g, na_w_qkv, na_rpb, na_w_o,
               pool_w, pool_scale, gla_w_in, gla_w_gate_f, gla_b_gate_f, gla_w_gate_b, gla_b_gate_b,
               gla_norm_g, gla_w_o, router_w, moe_w_gate, moe_w_up, moe_w_down)
    return y[:n_p].reshape(x_prompt.shape), y[n_p:].reshape(x_sample.shape)
```
